```python
import jax, jax.numpy as jnp
from jax import lax
import numpy as np

D_MODEL = 2048
BATCH = 4
SEQ = 2048
DEPTH = 4
DEC_BATCH = 8
DEC_SEQ = 8
PAST_LEN = 16384
PAGE_SIZE = 128

HEAD_DIM = 128
N_A_LAYERS = DEPTH // 2
N_B_LAYERS = DEPTH - N_A_LAYERS
N_HEADS_A = D_MODEL // HEAD_DIM
MOBA_BLOCK = 256
MOBA_TOPK = 3
MOBA_Q_CHUNK = 16
DIL_GROUPS = ((128, 1), (512, 4), (2048, 16))
N_GROUPS = len(DIL_GROUPS)
HEADS_PER_GROUP = D_MODEL // HEAD_DIM
N_KV_B = 4
DIL_Q_BLOCK = 128
DIL_MAX_WINDOW = max(w for w, _ in DIL_GROUPS)
D_FF = -(-(8 * D_MODEL) // (3 * 256)) * 256
ROPE_THETA = 10000.0
EPS = 1e-6
NEG = -1e30
F32 = jnp.float32

kernel_name = 'yoco_moba_dilated_decoder_step'


def rmsnorm(x, g):
    xf = x.astype(F32)
    y = xf * lax.rsqrt(jnp.mean(xf * xf, axis=-1, keepdims=True) + EPS)
    return (y * g.astype(F32)).astype(x.dtype)


def rope_tables(pos):
    inv_freq = ROPE_THETA ** (-jnp.arange(0, HEAD_DIM, 2, dtype=F32) / HEAD_DIM)
    ang = pos.astype(F32)[:, None] * inv_freq[None, :]
    return jnp.cos(ang), jnp.sin(ang)


def apply_rope(x, cos, sin):
    xf = x.astype(F32)
    x1, x2 = jnp.split(xf, 2, axis=-1)
    c = cos[:, None, :]
    s = sin[:, None, :]
    return jnp.concatenate([x1 * c - x2 * s, x2 * c + x1 * s], axis=-1).astype(x.dtype)


def swiglu_block(h, g, w_gu, w_down):
    gate, up = jnp.split(rmsnorm(h, g) @ w_gu, 2, axis=-1)
    return h + (jax.nn.silu(gate) * up) @ w_down


def moba_qkv(h, g, w_qkv, g_q, g_k, cos, sin):
    b, s, _ = h.shape
    qkv = (rmsnorm(h, g) @ w_qkv).reshape(b, s, 3, N_HEADS_A, HEAD_DIM)
    q = apply_rope(rmsnorm(qkv[:, :, 0], g_q), cos, sin)
    k = apply_rope(rmsnorm(qkv[:, :, 1], g_k), cos, sin)
    return q, k, qkv[:, :, 2]


def moba_core(q, k_sel, v_sel, sel_valid, k_own, v_own, own_mask):
    scale = HEAD_DIM ** -0.5
    s_sel = jnp.einsum('bqhd,bhqnd->bhqn', q, k_sel, preferred_element_type=F32) * scale
    s_own = jnp.einsum('bqhd,bkhd->bhqk', q, k_own, preferred_element_type=F32) * scale
    s_sel = jnp.where(sel_valid, s_sel, NEG)
    s_own = jnp.where(own_mask[None, None], s_own, NEG)
    p = jax.nn.softmax(jnp.concatenate([s_sel, s_own], axis=-1), axis=-1)
    ns = s_sel.shape[-1]
    o = (jnp.einsum('bhqn,bhqnd->bqhd', p[..., :ns], v_sel.astype(F32))
         + jnp.einsum('bhqk,bkhd->bqhd', p[..., ns:], v_own.astype(F32)))
    return o.astype(q.dtype)


def moba_prompt(q, k, v):
    b, s, h, d = q.shape
    nbf = s // MOBA_BLOCK
    n_sel = min(MOBA_TOPK, nbf)
    s_pad = -(-s // MOBA_BLOCK) * MOBA_BLOCK
    pad = ((0, 0), (0, s_pad - s), (0, 0), (0, 0))
    kp = jnp.pad(k, pad)
    vp = jnp.pad(v, pad)
    k_mean = kp[:, :nbf * MOBA_BLOCK].astype(F32).reshape(b, nbf, MOBA_BLOCK, h, d).mean(axis=2)
    n_chunks = s // MOBA_Q_CHUNK
    q_c = q.reshape(b, n_chunks, MOBA_Q_CHUNK, h, d).swapaxes(0, 1)
    b_idx = jnp.arange(b)[:, None, None, None, None]
    h_idx = jnp.arange(h)[None, :, None, None, None]

    def one_chunk(args):
        qi, ci = args
        qpos = ci * MOBA_Q_CHUNK + jnp.arange(MOBA_Q_CHUNK)
        own_start = (ci * MOBA_Q_CHUNK // MOBA_BLOCK) * MOBA_BLOCK
        k_own = lax.dynamic_slice_in_dim(kp, own_start, MOBA_BLOCK, axis=1)
        v_own = lax.dynamic_slice_in_dim(vp, own_start, MOBA_BLOCK, axis=1)
        own_mask = (own_start + jnp.arange(MOBA_BLOCK))[None, :] <= qpos[:, None]
        if n_sel > 0:
            gate = jnp.einsum('bqhd,bnhd->bhqn', qi, k_mean, preferred_element_type=F32)
            past = jnp.arange(nbf)[None, :] < (qpos // MOBA_BLOCK)[:, None]
            gate = jnp.where(past[None, None], gate, -jnp.inf)
            g_val, blk = lax.top_k(gate, n_sel)
            valid = jnp.isfinite(g_val)
            tok = blk[..., None] * MOBA_BLOCK + jnp.arange(MOBA_BLOCK)
            k_sel = kp[b_idx, tok, h_idx].reshape(b, h, MOBA_Q_CHUNK, n_sel * MOBA_BLOCK, d)
            v_sel = vp[b_idx, tok, h_idx].reshape(b, h, MOBA_Q_CHUNK, n_sel * MOBA_BLOCK, d)
            sel_valid = jnp.broadcast_to(valid[..., None], valid.shape + (MOBA_BLOCK,)).reshape(
                b, h, MOBA_Q_CHUNK, n_sel * MOBA_BLOCK)
        else:
            k_sel = jnp.zeros((b, h, MOBA_Q_CHUNK, 0, d), q.dtype)
            v_sel = k_sel
            sel_valid = jnp.zeros((b, h, MOBA_Q_CHUNK, 0), bool)
        return moba_core(qi, k_sel, v_sel, sel_valid, k_own, v_own, own_mask)

    o = lax.map(one_chunk, (q_c, jnp.arange(n_chunks)))
    return o.swapaxes(0, 1).reshape(b, s, h, d)


def moba_sample(q, k_new, v_new, cache_k, cache_v, layer, page_table):
    db, t, h, d = q.shape
    ppb = MOBA_BLOCK // PAGE_SIZE
    nbf = PAST_LEN // MOBA_BLOCK
    n_sel = min(MOBA_TOPK, nbf)
    part = (PAST_LEN - nbf * MOBA_BLOCK) // PAGE_SIZE
    qpos = PAST_LEN + jnp.arange(t)
    own_pages = page_table[:, nbf * ppb: nbf * ppb + part]
    k_part = cache_k[layer, own_pages].reshape(db, part * PAGE_SIZE, h, d).astype(k_new.dtype)
    v_part = cache_v[layer, own_pages].reshape(db, part * PAGE_SIZE, h, d).astype(v_new.dtype)
    k_own = jnp.concatenate([k_part, k_new], axis=1)
    v_own = jnp.concatenate([v_part, v_new], axis=1)
    own_pos = jnp.concatenate([nbf * MOBA_BLOCK + jnp.arange(part * PAGE_SIZE), qpos])
    own_mask = own_pos[None, :] <= qpos[:, None]
    if n_sel > 0:
        k_past = cache_k[layer, page_table[:, :nbf * ppb]]
        k_mean = k_past.astype(F32).reshape(db, nbf, MOBA_BLOCK, h, d).mean(axis=2)
        gate = jnp.einsum('bqhd,bnhd->bhqn', q, k_mean, preferred_element_type=F32)
        _, blk = lax.top_k(gate, n_sel)
        logical = blk[..., None] * ppb + jnp.arange(ppb)
        phys = page_table[jnp.arange(db)[:, None, None, None, None], logical][..., None]
        slot = jnp.arange(PAGE_SIZE)
        hh = jnp.arange(h)[None, :, None, None, None, None]
        k_sel = cache_k[layer, phys, slot, hh].reshape(db, h, t, n_sel * MOBA_BLOCK, d)
        v_sel = cache_v[layer, phys, slot, hh].reshape(db, h, t, n_sel * MOBA_BLOCK, d)
        sel_valid = jnp.ones((db, h, t, n_sel * MOBA_BLOCK), bool)
    else:
        k_sel = jnp.zeros((db, h, t, 0, d), q.dtype)
        v_sel = k_sel
        sel_valid = jnp.zeros((db, h, t, 0), bool)
    return moba_core(q, k_sel, v_sel, sel_valid, k_own, v_own, own_mask)


def shared_kv(h, g_kv, w_kv, g_k, cos, sin):
    b, s, _ = h.shape
    kv = (rmsnorm(h, g_kv) @ w_kv).reshape(b, s, 2, N_KV_B, HEAD_DIM)
    k = apply_rope(rmsnorm(kv[:, :, 0], g_k), cos, sin)
    return k, kv[:, :, 1]


def dil_query(h, g, w_q, g_q, cos, sin):
    b, s, _ = h.shape
    q = (rmsnorm(h, g) @ w_q).reshape(b, s, N_GROUPS * HEADS_PER_GROUP, HEAD_DIM)
    return apply_rope(rmsnorm(q, g_q), cos, sin)


def dilated_attn(q, k_all, v_all, q_idx):
    b, nq, _, d = q.shape
    rep = HEADS_PER_GROUP // N_KV_B
    scale = HEAD_DIM ** -0.5
    outs, lses = [], []
    for g, (win, dil) in enumerate(DIL_GROUPS):
        qg = q[:, :, g * HEADS_PER_GROUP:(g + 1) * HEADS_PER_GROUP].reshape(b, nq, N_KV_B, rep, d)
        dist = dil * jnp.arange(win // dil + 1)
        kidx = q_idx[:, None] - dist[None, :]
        valid = kidx >= 0
        kidx = jnp.maximum(kidx, 0)
        kg = k_all[:, kidx]
        vg = v_all[:, kidx]
        s = jnp.einsum('bqgrd,bqkgd->bqgrk', qg, kg, preferred_element_type=F32) * scale
        s = jnp.where(valid[None, :, None, None, :], s, NEG)
        lse = jax.nn.logsumexp(s, axis=-1)
        p = jnp.exp(s - lse[..., None])
        o = jnp.einsum('bqgrk,bqkgd->bqgrd', p, vg.astype(F32))
        outs.append(o.reshape(b, nq, HEADS_PER_GROUP, d))
        lses.append(lse.reshape(b, nq, HEADS_PER_GROUP))
    w = jax.nn.softmax(jnp.stack(lses, axis=0), axis=0)
    o = jnp.einsum('gbqh,gbqhd->bqhd', w, jnp.stack(outs, axis=0))
    return o.astype(q.dtype)


def dilated_prompt(q, k, v):
    b, s, hq, d = q.shape
    nblk = s // DIL_Q_BLOCK
    qb = q.reshape(b, nblk, DIL_Q_BLOCK, hq, d).swapaxes(0, 1)

    def one_block(args):
        qi, bi = args
        return dilated_attn(qi, k, v, bi * DIL_Q_BLOCK + jnp.arange(DIL_Q_BLOCK))

    o = lax.map(one_block, (qb, jnp.arange(nblk)))
    return o.swapaxes(0, 1).reshape(b, s, HEADS_PER_GROUP, d)


def setup_inputs(seed: int = 0) -> dict:
    key = jax.random.key(seed)
    ks = jax.random.split(key, 24)
    n_pages = PAST_LEN // PAGE_SIZE
    n_pool = (5 * DEC_BATCH * n_pages + 3) // 4
    buf = min(DIL_MAX_WINDOW, PAST_LEN)

    def nrm(k, shape):
        return jax.random.normal(k, shape, F32)

    def lin(k, shape):
        return nrm(k, shape) * (shape[-2] ** -0.5)

    def gain(k, shape):
        return 1.0 + 0.02 * nrm(k, shape)

    qa_cols = 3 * N_HEADS_A * HEAD_DIM
    qb_cols = N_GROUPS * HEADS_PER_GROUP * HEAD_DIM
    page_table = jax.random.permutation(ks[6], n_pool)[:DEC_BATCH * n_pages].reshape(
        DEC_BATCH, n_pages).astype(jnp.int32)
    return {
        'x_prompt': nrm(ks[0], (BATCH, SEQ, D_MODEL)),
        'x_sample': nrm(ks[1], (DEC_BATCH, DEC_SEQ, D_MODEL)),
        'cache_k_a': nrm(ks[2], (N_A_LAYERS, n_pool, PAGE_SIZE, N_HEADS_A, HEAD_DIM)),
        'cache_v_a': nrm(ks[3], (N_A_LAYERS, n_pool, PAGE_SIZE, N_HEADS_A, HEAD_DIM)),
        'state_k_b': nrm(ks[4], (DEC_BATCH, buf, N_KV_B, HEAD_DIM)),
        'state_v_b': nrm(ks[5], (DEC_BATCH, buf, N_KV_B, HEAD_DIM)),
        'page_table': page_table,
        'g_attn_a': gain(ks[7], (N_A_LAYERS, D_MODEL)),
        'w_qkv_a': lin(ks[8], (N_A_LAYERS, D_MODEL, qa_cols)),
        'g_q_a': gain(ks[9], (N_A_LAYERS, HEAD_DIM)),
        'g_k_a': gain(ks[10], (N_A_LAYERS, HEAD_DIM)),
        'w_o_a': lin(ks[11], (N_A_LAYERS, N_HEADS_A * HEAD_DIM, D_MODEL)),
        'g_kv_b': gain(ks[12], (D_MODEL,)),
        'w_kv_b': lin(ks[13], (D_MODEL, 2 * N_KV_B * HEAD_DIM)),
        'g_k_b': gain(ks[14], (HEAD_DIM,)),
        'g_attn_b': gain(ks[15], (N_B_LAYERS, D_MODEL)),
        'w_q_b': lin(ks[16], (N_B_LAYERS, D_MODEL, qb_cols)),
        'g_q_b': gain(ks[17], (N_B_LAYERS, HEAD_DIM)),
        'w_o_b': lin(ks[18], (N_B_LAYERS, HEADS_PER_GROUP * HEAD_DIM, D_MODEL)),
        'g_ffn': gain(ks[19], (DEPTH, D_MODEL)),
        'w_gu': lin(ks[20], (DEPTH, D_MODEL, 2 * D_FF)),
        'w_down': lin(ks[21], (DEPTH, D_FF, D_MODEL)),
    }


def reference(x_prompt, x_sample, cache_k_a, cache_v_a, state_k_b, state_v_b, page_table,
              g_attn_a, w_qkv_a, g_q_a, g_k_a, w_o_a,
              g_kv_b, w_kv_b, g_k_b, g_attn_b, w_q_b, g_q_b, w_o_b,
              g_ffn, w_gu, w_down):
    bp, sp, _ = x_prompt.shape
    bs, ss, _ = x_sample.shape
    cos_p, sin_p = rope_tables(jnp.arange(sp))
    cos_s, sin_s = rope_tables(PAST_LEN + jnp.arange(ss))
    hp, hs = x_prompt, x_sample
    ka_p, va_p, ka_s, va_s = [], [], [], []
    for l in range(DEPTH):
        if l < N_A_LAYERS:
            qp, kp, vp = moba_qkv(hp, g_attn_a[l], w_qkv_a[l], g_q_a[l], g_k_a[l], cos_p, sin_p)
            qs, ks, vs = moba_qkv(hs, g_attn_a[l], w_qkv_a[l], g_q_a[l], g_k_a[l], cos_s, sin_s)
            op = moba_prompt(qp, kp, vp)
            os_ = moba_sample(qs, ks, vs, cache_k_a, cache_v_a, l, page_table)
            hp = hp + op.reshape(bp, sp, -1) @ w_o_a[l]
            hs = hs + os_.reshape(bs, ss, -1) @ w_o_a[l]
            ka_p.append(kp)
            va_p.append(vp)
            ka_s.append(ks)
            va_s.append(vs)
        else:
            if l == N_A_LAYERS:
                kbp, vbp = shared_kv(hp, g_kv_b, w_kv_b, g_k_b, cos_p, sin_p)
                kbs, vbs = shared_kv(hs, g_kv_b, w_kv_b, g_k_b, cos_s, sin_s)
                k_all_s = jnp.concatenate([state_k_b.astype(kbs.dtype), kbs], axis=1)
                v_all_s = jnp.concatenate([state_v_b.astype(vbs.dtype), vbs], axis=1)
            j = l - N_A_LAYERS
            qp = dil_query(hp, g_attn_b[j], w_q_b[j], g_q_b[j], cos_p, sin_p)
            qs = dil_query(hs, g_attn_b[j], w_q_b[j], g_q_b[j], cos_s, sin_s)
            op = dilated_prompt(qp, kbp, vbp)
            os_ = dilated_attn(qs, k_all_s, v_all_s, state_k_b.shape[1] + jnp.arange(ss))
            hp = hp + op.reshape(bp, sp, -1) @ w_o_b[j]
            hs = hs + os_.reshape(bs, ss, -1) @ w_o_b[j]
        hp = swiglu_block(hp, g_ffn[l], w_gu[l], w_down[l])
        hs = swiglu_block(hs, g_ffn[l], w_gu[l], w_down[l])
    buf_p = min(DIL_MAX_WINDOW, sp)
    return (hp, hs, jnp.stack(ka_p), jnp.stack(va_p), jnp.stack(ka_s), jnp.stack(va_s),
            kbp[:, sp - buf_p:], vbp[:, sp - buf_p:], k_all_s[:, ss:], v_all_s[:, ss:])
```

```python
import functools

import jax
import jax.numpy as jnp
from jax import lax
from jax.experimental import pallas as pl
from jax.experimental.pallas import tpu as pltpu

F32 = jnp.float32
BF16 = jnp.bfloat16

D_MODEL = 2048
HEAD_DIM = 128
N_HEADS = D_MODEL // HEAD_DIM
PAST_LEN = 16384
PAGE_SIZE = 128
MOBA_BLOCK = 256
MOBA_TOPK = 3
DIL_GROUPS = ((128, 1), (512, 4), (2048, 16))
N_KV_B = 4
KV_REP = N_HEADS // N_KV_B
ROPE_THETA = 10000.0
EPS = 1e-6
NEG = -1e30
SCALE = HEAD_DIM ** -0.5

VMEM_LIMIT_V7X = 56 * 1024 * 1024


def _params(*sem):
    return pltpu.CompilerParams(dimension_semantics=sem, vmem_limit_bytes=VMEM_LIMIT_V7X)


def _rms_scale(x):
    return lax.rsqrt(jnp.mean(x * x, axis=-1, keepdims=True) + EPS)


def _head_norm_rope(acc, gain, cos, sin):
    outs = []
    for h in range(acc.shape[1] // HEAD_DIM):
        a = acc[:, h * HEAD_DIM:(h + 1) * HEAD_DIM]
        y = a * _rms_scale(a) * gain
        outs.append(y * cos + pltpu.roll(y, HEAD_DIM // 2, axis=1) * sin)
    return outs[0] if len(outs) == 1 else jnp.concatenate(outs, axis=1)


def _proj_kernel(x_ref, g_ref, w_ref, gain_ref, cos_ref, sin_ref, *rest, nj, outs, n_alias):
    out_refs = rest[n_alias:n_alias + len(outs)]
    xn_ref = rest[n_alias + len(outs)]
    c = pl.program_id(1)

    @pl.when(c == 0)
    def _():
        x = x_ref[...]
        xn_ref[...] = (x * _rms_scale(x) * g_ref[...]).astype(BF16)

    acc = jnp.dot(xn_ref[...], w_ref[...], preferred_element_type=F32)
    sections = sorted({sec for sec, _, _, _ in outs})
    for sec in sections:
        rope = [r for s, _, r, _ in outs if s == sec][0]

        @pl.when(c // nj == sec)
        def _(sec=sec, rope=rope):
            y = _head_norm_rope(acc, gain_ref[...], cos_ref[...], sin_ref[...]) if rope else acc
            for o_ref, (s, dt, _, _) in zip(out_refs, outs):
                if s == sec:
                    o_ref[...] = y.astype(dt)


def _proj(x, g, w, gains, cos, sin, *, secw, outs, tm, tn, pos_rows, prev=None):
    m, d = x.shape
    nsec = w.shape[1] // secw
    nj = secw // tn
    ni = m // tm
    npos = pos_rows // tm
    prev = prev or {}

    def out_map(sec, layer):
        def f(i, c):
            jj = jnp.clip(c - sec * nj, 0, nj - 1)
            return (i, jj) if layer is None else (layer, i, jj)
        return f

    out_shape, out_specs, aliases, alias_inputs = [], [], {}, []
    for k, (sec, dt, _, layer) in enumerate(outs):
        if layer is None:
            out_shape.append(jax.ShapeDtypeStruct((m, secw), dt))
            out_specs.append(pl.BlockSpec((tm, tn), out_map(sec, None)))
        else:
            out_shape.append(jax.ShapeDtypeStruct((2, m, secw), dt))
            out_specs.append(pl.BlockSpec((None, tm, tn), out_map(sec, layer)))
            if k in prev:
                aliases[6 + len(alias_inputs)] = k
                alias_inputs.append(prev[k])
    in_specs = [
        pl.BlockSpec((tm, d), lambda i, c: (i, 0)),
        pl.BlockSpec((1, d), lambda i, c: (0, 0)),
        pl.BlockSpec((d, tn), lambda i, c: (0, c)),
        pl.BlockSpec((None, 1, HEAD_DIM), lambda i, c: (c // nj, 0, 0)),
        pl.BlockSpec((tm, HEAD_DIM), lambda i, c: (i % npos, 0)),
        pl.BlockSpec((tm, HEAD_DIM), lambda i, c: (i % npos, 0)),
    ] + [pl.BlockSpec(memory_space=pl.ANY)] * len(alias_inputs)
    return pl.pallas_call(
        functools.partial(_proj_kernel, nj=nj, outs=outs, n_alias=len(alias_inputs)),
        grid=(ni, nsec * nj),
        in_specs=in_specs,
        out_specs=out_specs,
        out_shape=out_shape,
        scratch_shapes=[pltpu.VMEM((tm, d), BF16)],
        input_output_aliases=aliases,
        compiler_params=_params("parallel", "arbitrary"),
    )(x, g.reshape(1, d), w, gains, cos, sin, *alias_inputs)


def _oproj_kernel(x_ref, w_ref, r_ref, o_ref):
    o_ref[...] = r_ref[...] + jnp.dot(x_ref[...].astype(BF16), w_ref[...], preferred_element_type=F32)


def _oproj(x, w, res, *, tm):
    m, d = x.shape
    n = w.shape[1]
    return pl.pallas_call(
        _oproj_kernel,
        grid=(m // tm,),
        in_specs=[pl.BlockSpec((tm, d), lambda i: (i, 0)),
                  pl.BlockSpec((d, n), lambda i: (0, 0)),
                  pl.BlockSpec((tm, n), lambda i: (i, 0))],
        out_specs=pl.BlockSpec((tm, n), lambda i: (i, 0)),
        out_shape=jax.ShapeDtypeStruct((m, n), F32),
        compiler_params=_params("parallel"),
    )(x, w, res)


def _merge_oproj_kernel(o0_ref, o1_ref, o2_ref, l0_ref, l1_ref, l2_ref, w_ref, r_ref, o_ref):
    l0, l1, l2 = l0_ref[...], l1_ref[...], l2_ref[...]
    mx = jnp.maximum(jnp.maximum(l0, l1), l2)
    e0, e1, e2 = jnp.exp(l0 - mx), jnp.exp(l1 - mx), jnp.exp(l2 - mx)
    inv = 1.0 / (e0 + e1 + e2)
    w0, w1, w2 = e0 * inv, e1 * inv, e2 * inv
    cols = []
    for h in range(N_HEADS):
        sl = slice(h * HEAD_DIM, (h + 1) * HEAD_DIM)
        cols.append((w0[:, h:h + 1] * o0_ref[:, sl] + w1[:, h:h + 1] * o1_ref[:, sl]
                     + w2[:, h:h + 1] * o2_ref[:, sl]).astype(BF16))
    x = jnp.concatenate(cols, axis=1)
    o_ref[...] = r_ref[...] + jnp.dot(x, w_ref[...], preferred_element_type=F32)


def _merge_oproj(o_parts, lse_parts, w, res, *, tm):
    m, d = res.shape
    row = lambda i: (i, 0)
    return pl.pallas_call(
        _merge_oproj_kernel,
        grid=(m // tm,),
        in_specs=[pl.BlockSpec((tm, d), row)] * 3 + [pl.BlockSpec((tm, N_HEADS), row)] * 3
        + [pl.BlockSpec((d, d), lambda i: (0, 0)), pl.BlockSpec((tm, d), row)],
        out_specs=pl.BlockSpec((tm, d), row),
        out_shape=jax.ShapeDtypeStruct((m, d), F32),
        compiler_params=_params("parallel"),
    )(*o_parts, *lse_parts, w, res)


def _ffn_kernel(x_ref, g_ref, wg_ref, wu_ref, wd_ref, o_ref, xn_ref, acc_ref):
    j = pl.program_id(1)

    @pl.when(j == 0)
    def _():
        x = x_ref[...]
        xn_ref[...] = (x * _rms_scale(x) * g_ref[...]).astype(BF16)
        acc_ref[...] = jnp.zeros_like(acc_ref)

    xn = xn_ref[...]
    gate = jnp.dot(xn, wg_ref[...], preferred_element_type=F32)
    up = jnp.dot(xn, wu_ref[...], preferred_element_type=F32)
    act = (gate * jax.nn.sigmoid(gate) * up).astype(BF16)
    acc_ref[...] += jnp.dot(act, wd_ref[...], preferred_element_type=F32)

    @pl.when(j == pl.num_programs(1) - 1)
    def _():
        o_ref[...] = x_ref[...] + acc_ref[...]


def _ffn(x, g, w_gu, w_down, *, tm, tf):
    m, d = x.shape
    dff = w_down.shape[0]
    nf = dff // tf
    return pl.pallas_call(
        _ffn_kernel,
        grid=(m // tm, nf),
        in_specs=[pl.BlockSpec((tm, d), lambda i, j: (i, 0)),
                  pl.BlockSpec((1, d), lambda i, j: (0, 0)),
                  pl.BlockSpec((d, tf), lambda i, j: (0, j)),
                  pl.BlockSpec((d, tf), lambda i, j: (0, j + nf)),
                  pl.BlockSpec((tf, d), lambda i, j: (j, 0))],
        out_specs=pl.BlockSpec((tm, d), lambda i, j: (i, 0)),
        out_shape=jax.ShapeDtypeStruct((m, d), F32),
        scratch_shapes=[pltpu.VMEM((tm, d), BF16), pltpu.VMEM((tm, d), F32)],
        compiler_params=_params("parallel", "arbitrary"),
    )(x, g.reshape(1, d), w_gu, w_gu, w_down)


def _moba_prompt_kernel(q_ref, k_ref, v_ref, o_ref, *, seq):
    nb = seq // MOBA_BLOCK
    kb = k_ref[...].astype(BF16)
    vb = v_ref[...].astype(BF16)
    kmean = [jnp.mean(k_ref[j * MOBA_BLOCK:(j + 1) * MOBA_BLOCK, :], axis=0, keepdims=True)
             for j in range(nb - 1)]
    row = lax.broadcasted_iota(jnp.int32, (MOBA_BLOCK, MOBA_BLOCK), 0)
    col = lax.broadcasted_iota(jnp.int32, (MOBA_BLOCK, MOBA_BLOCK), 1)
    causal = col <= row
    for i in range(nb):
        q = q_ref[i * MOBA_BLOCK:(i + 1) * MOBA_BLOCK, :]
        n = (i + 1) * MOBA_BLOCK
        s = lax.dot_general(q.astype(BF16), kb[:n], (((1,), (1,)), ((), ())),
                            preferred_element_type=F32) * SCALE
        tiles = []
        if i > MOBA_TOPK:
            gate = [jnp.sum(q * kmean[j], axis=-1, keepdims=True) for j in range(i)]
            for j in range(i):
                rank = jnp.zeros((MOBA_BLOCK, 1), F32)
                for j2 in range(i):
                    if j2 < j:
                        rank += jnp.where(gate[j2] >= gate[j], 1.0, 0.0)
                    elif j2 > j:
                        rank += jnp.where(gate[j2] > gate[j], 1.0, 0.0)
                tiles.append(jnp.where(rank < MOBA_TOPK, s[:, j * MOBA_BLOCK:(j + 1) * MOBA_BLOCK], NEG))
        else:
            tiles = [s[:, j * MOBA_BLOCK:(j + 1) * MOBA_BLOCK] for j in range(i)]
        tiles.append(jnp.where(causal, s[:, i * MOBA_BLOCK:], NEG))
        sm = tiles[0] if len(tiles) == 1 else jnp.concatenate(tiles, axis=1)
        mx = jnp.max(sm, axis=-1, keepdims=True)
        p = jnp.exp(sm - mx)
        den = jnp.sum(p, axis=-1, keepdims=True)
        o = jnp.dot(p.astype(BF16), vb[:n], preferred_element_type=F32) * (1.0 / den)
        o_ref[i * MOBA_BLOCK:(i + 1) * MOBA_BLOCK, :] = o.astype(o_ref.dtype)


def _moba_prompt(q, kbuf, vbuf, layer, *, batch, seq):
    m = q.shape[0]
    return pl.pallas_call(
        functools.partial(_moba_prompt_kernel, seq=seq),
        grid=(batch, N_HEADS),
        in_specs=[pl.BlockSpec((seq, HEAD_DIM), lambda b, h: (b, h)),
                  pl.BlockSpec((None, seq, HEAD_DIM), lambda b, h: (layer, b, h)),
                  pl.BlockSpec((None, seq, HEAD_DIM), lambda b, h: (layer, b, h))],
        out_specs=pl.BlockSpec((seq, HEAD_DIM), lambda b, h: (b, h)),
        out_shape=jax.ShapeDtypeStruct((m, D_MODEL), BF16),
        compiler_params=_params("parallel", "parallel"),
    )(q, kbuf, vbuf)


PAGES_PER_BLOCK = MOBA_BLOCK // PAGE_SIZE
N_PAST_BLOCKS = PAST_LEN // MOBA_BLOCK
GATE_BLOCKS_PER_STEP = 4
GATE_PAGES_PER_STEP = GATE_BLOCKS_PER_STEP * PAGES_PER_BLOCK


def _moba_gate_kernel(pt_ref, q_ref, *rest, dec_seq):
    pages = rest[:GATE_PAGES_PER_STEP]
    idx_ref = rest[GATE_PAGES_PER_STEP]
    kmean_ref = rest[GATE_PAGES_PER_STEP + 1]
    step = pl.program_id(1)
    for p in range(GATE_BLOCKS_PER_STEP):
        tot = jnp.zeros((1, D_MODEL), F32)
        for u in range(PAGES_PER_BLOCK):
            tot += jnp.sum(pages[p * PAGES_PER_BLOCK + u][...], axis=0, keepdims=True)
        kmean_ref[pl.ds(step * GATE_BLOCKS_PER_STEP + p, 1), :] = tot * (1.0 / MOBA_BLOCK)

    @pl.when(step == pl.num_programs(1) - 1)
    def _():
        km = kmean_ref[...]
        lane = lax.broadcasted_iota(jnp.int32, (N_PAST_BLOCKS, HEAD_DIM), 1)
        blk = lax.broadcasted_iota(jnp.int32, (N_PAST_BLOCKS, HEAD_DIM), 0).astype(F32)
        for t in range(dec_seq):
            prod = km * q_ref[pl.ds(t, 1), :]
            gate = jnp.full((N_PAST_BLOCKS, HEAD_DIM), -jnp.inf, F32)
            for h in range(N_HEADS):
                gh = jnp.sum(prod[:, h * HEAD_DIM:(h + 1) * HEAD_DIM], axis=-1, keepdims=True)
                gate = jnp.where(lane == h, gh, gate)
            for k in range(MOBA_TOPK):
                mx = jnp.max(gate, axis=0, keepdims=True)
                pick = jnp.min(jnp.where(gate == mx, blk, float(N_PAST_BLOCKS)), axis=0, keepdims=True)
                idx_ref[pl.ds(t * MOBA_TOPK + k, 1), :] = pick.astype(jnp.int32)
                gate = jnp.where(blk == pick, -jnp.inf, gate)


def _moba_gate(q3, cache_k4, page_flat, layer, *, dec_batch, dec_seq, n_pages):
    def page_map(u):
        return lambda b, s, pt: (layer, pt[b * n_pages + s * GATE_PAGES_PER_STEP + u], 0, 0)

    grid_spec = pltpu.PrefetchScalarGridSpec(
        num_scalar_prefetch=1,
        grid=(dec_batch, N_PAST_BLOCKS // GATE_BLOCKS_PER_STEP),
        in_specs=[pl.BlockSpec((None, dec_seq, D_MODEL), lambda b, s, pt: (b, 0, 0))]
        + [pl.BlockSpec((None, None, PAGE_SIZE, D_MODEL), page_map(u)) for u in range(GATE_PAGES_PER_STEP)],
        out_specs=pl.BlockSpec((None, dec_seq * MOBA_TOPK, HEAD_DIM), lambda b, s, pt: (b, 0, 0)),
        scratch_shapes=[pltpu.VMEM((N_PAST_BLOCKS, D_MODEL), F32)],
    )
    return pl.pallas_call(
        functools.partial(_moba_gate_kernel, dec_seq=dec_seq),
        grid_spec=grid_spec,
        out_shape=jax.ShapeDtypeStruct((dec_batch, dec_seq * MOBA_TOPK, HEAD_DIM), jnp.int32),
        compiler_params=_params("parallel", "arbitrary"),
    )(page_flat, q3, *([cache_k4] * GATE_PAGES_PER_STEP))


N_SEL_PAGES = MOBA_TOPK * PAGES_PER_BLOCK


def _moba_sample_kernel(pt_ref, idx_ref, q_ref, kn_ref, vn_ref, *rest, dec_seq):
    kp = rest[:N_SEL_PAGES]
    vp = rest[N_SEL_PAGES:2 * N_SEL_PAGES]
    o_ref = rest[2 * N_SEL_PAGES]
    t = pl.program_id(2)
    q = q_ref[pl.ds(t, 1), :]
    s_sel = [jnp.sum(k[...] * q, axis=-1, keepdims=True) * SCALE for k in kp]
    tpos = lax.broadcasted_iota(jnp.int32, (dec_seq, 1), 0)
    s_own = jnp.where(tpos <= t, jnp.sum(kn_ref[...] * q, axis=-1, keepdims=True) * SCALE, NEG)
    mx = jnp.max(s_own, axis=0, keepdims=True)
    for s in s_sel:
        mx = jnp.maximum(mx, jnp.max(s, axis=0, keepdims=True))
    p_own = jnp.exp(s_own - mx)
    den = jnp.sum(p_own, axis=0, keepdims=True)
    acc = jnp.sum(p_own * vn_ref[...], axis=0, keepdims=True)
    for s, v in zip(s_sel, vp):
        p = jnp.exp(s - mx)
        den += jnp.sum(p, axis=0, keepdims=True)
        acc += jnp.sum(p * v[...], axis=0, keepdims=True)
    o_ref[pl.ds(t, 1), :] = acc * (1.0 / den)


def _moba_sample(q3, knew4, vnew4, cache_k4, cache_v4, page_flat, idx_flat, layer, *, dec_batch, dec_seq, n_pages):
    def sel_map(u):
        k, half = divmod(u, PAGES_PER_BLOCK)

        def f(b, h, t, pt, idx):
            blk = idx[((b * dec_seq + t) * MOBA_TOPK + k) * N_HEADS + h]
            return (layer, pt[b * n_pages + blk * PAGES_PER_BLOCK + half], 0, h)
        return f

    own = pl.BlockSpec((None, None, dec_seq, HEAD_DIM), lambda b, h, t, pt, idx: (layer, b, 0, h))
    grid_spec = pltpu.PrefetchScalarGridSpec(
        num_scalar_prefetch=2,
        grid=(dec_batch, N_HEADS, dec_seq),
        in_specs=[pl.BlockSpec((None, dec_seq, HEAD_DIM), lambda b, h, t, pt, idx: (b, 0, h)), own, own]
        + [pl.BlockSpec((None, None, PAGE_SIZE, HEAD_DIM), sel_map(u)) for u in range(N_SEL_PAGES)] * 2,
        out_specs=pl.BlockSpec((None, dec_seq, HEAD_DIM), lambda b, h, t, pt, idx: (b, 0, h)),
    )
    return pl.pallas_call(
        functools.partial(_moba_sample_kernel, dec_seq=dec_seq),
        grid_spec=grid_spec,
        out_shape=jax.ShapeDtypeStruct((dec_batch, dec_seq, D_MODEL), F32),
        compiler_params=_params("parallel", "parallel", "arbitrary"),
    )(page_flat, idx_flat, q3, knew4, vnew4, *([cache_k4] * N_SEL_PAGES), *([cache_v4] * N_SEL_PAGES))


DIL_TILE = 128


def _dil_tile(qt, kw, vw, mask):
    qs = jnp.concatenate([qt[:, r * HEAD_DIM:(r + 1) * HEAD_DIM] for r in range(KV_REP)], axis=0)
    s = lax.dot_general(qs, kw, (((1,), (1,)), ((), ())), preferred_element_type=F32) * SCALE
    s = jnp.where(mask, s, NEG)
    mx = jnp.max(s, axis=-1, keepdims=True)
    p = jnp.exp(s - mx)
    den = jnp.sum(p, axis=-1, keepdims=True)
    o = jnp.dot(p.astype(BF16), vw, preferred_element_type=F32) * (1.0 / den)
    return o, mx + jnp.log(den)


def _dil_prompt_kernel(q_ref, k_ref, v_ref, o_ref, lse_ref, *, rows):
    nt = rows // DIL_TILE

    rep_lane = lax.broadcasted_iota(jnp.int32, (DIL_TILE, KV_REP), 1)

    def emit(start, o, lse):
        lse_cols = jnp.zeros((DIL_TILE, KV_REP), F32)
        for r in range(KV_REP):
            o_ref[pl.ds(start, DIL_TILE), r * HEAD_DIM:(r + 1) * HEAD_DIM] = o[r * DIL_TILE:(r + 1) * DIL_TILE]
            lse_cols = jnp.where(rep_lane == r, lse[r * DIL_TILE:(r + 1) * DIL_TILE], lse_cols)
        lse_ref[pl.ds(start, DIL_TILE), :] = lse_cols

    a0 = lax.broadcasted_iota(jnp.int32, (KV_REP * DIL_TILE, DIL_TILE), 0) & (DIL_TILE - 1)
    c0 = lax.broadcasted_iota(jnp.int32, (KV_REP * DIL_TILE, DIL_TILE), 1)
    o, lse = _dil_tile(q_ref[0:DIL_TILE, :], k_ref[0:DIL_TILE, :], v_ref[0:DIL_TILE, :], c0 <= a0)
    emit(0, o, lse)
    if nt > 1:
        a = lax.broadcasted_iota(jnp.int32, (KV_REP * DIL_TILE, 2 * DIL_TILE), 0) & (DIL_TILE - 1)
        c = lax.broadcasted_iota(jnp.int32, (KV_REP * DIL_TILE, 2 * DIL_TILE), 1)
        band = (c >= a) & (c <= a + DIL_TILE)

        def body(t, carry):
            start = pl.multiple_of(t * DIL_TILE, DIL_TILE)
            prev = pl.multiple_of((t - 1) * DIL_TILE, DIL_TILE)
            o, lse = _dil_tile(q_ref[pl.ds(start, DIL_TILE), :],
                               k_ref[pl.ds(prev, 2 * DIL_TILE), :],
                               v_ref[pl.ds(prev, 2 * DIL_TILE), :], band)
            emit(start, o, lse)
            return carry

        lax.fori_loop(1, nt, body, 0)


def _dil_prompt_group(qg, kb, vb, dil, *, batch, seq):
    rows = seq // dil
    qv = qg.reshape(batch, rows, dil * D_MODEL)
    kv = kb.reshape(batch, rows, dil * N_KV_B * HEAD_DIM)
    vv = vb.reshape(batch, rows, dil * N_KV_B * HEAD_DIM)
    wide = KV_REP * HEAD_DIM
    o, lse = pl.pallas_call(
        functools.partial(_dil_prompt_kernel, rows=rows),
        grid=(batch, dil, N_KV_B),
        in_specs=[pl.BlockSpec((None, rows, wide), lambda b, r, g: (b, 0, r * N_KV_B + g)),
                  pl.BlockSpec((None, rows, HEAD_DIM), lambda b, r, g: (b, 0, r * N_KV_B + g)),
                  pl.BlockSpec((None, rows, HEAD_DIM), lambda b, r, g: (b, 0, r * N_KV_B + g))],
        out_specs=[pl.BlockSpec((None, rows, wide), lambda b, r, g: (b, 0, r * N_KV_B + g)),
                   pl.BlockSpec((None, None, None, rows, KV_REP), lambda b, r, g: (b, r, g, 0, 0))],
        out_shape=[jax.ShapeDtypeStruct((batch, rows, dil * D_MODEL), F32),
                   jax.ShapeDtypeStruct((batch, dil, N_KV_B, rows, KV_REP), F32)],
        compiler_params=_params("parallel", "parallel", "parallel"),
    )(qv, kv, vv)
    lse = lse.transpose(0, 3, 1, 2, 4).reshape(batch * seq, N_HEADS)
    return o.reshape(batch * seq, D_MODEL), lse


def _dil_sample_kernel(q0_ref, q1_ref, q2_ref, ks_ref, vs_ref, kn_ref, vn_ref, o_ref, *, dec_seq, buf):
    q_refs = (q0_ref, q1_ref, q2_ref)
    ng = len(DIL_GROUPS)
    per_group = KV_REP * dec_seq
    nrow = ng * per_group
    pad = jnp.zeros((HEAD_DIM - dec_seq, HEAD_DIM), F32)

    def masks(ncol, is_state):
        rowi = lax.broadcasted_iota(jnp.int32, (nrow, ncol), 0)
        col = lax.broadcasted_iota(jnp.int32, (nrow, ncol), 1)
        t = rowi & (dec_seq - 1)
        dil = jnp.full((nrow, ncol), DIL_GROUPS[-1][1], jnp.int32)
        win = jnp.full((nrow, ncol), DIL_GROUPS[-1][0], jnp.int32)
        for g in range(ng - 2, -1, -1):
            in_g = rowi < (g + 1) * per_group
            dil = jnp.where(in_g, DIL_GROUPS[g][1], dil)
            win = jnp.where(in_g, DIL_GROUPS[g][0], win)
        dist = (buf + t - col) if is_state else (t - col)
        ok = (dist >= 0) & ((dist & (dil - 1)) == 0) & (dist <= win)
        if not is_state:
            ok = ok & (col < dec_seq)
        return ok

    m_state = masks(buf, True)
    m_new = masks(HEAD_DIM, False)
    for kv in range(N_KV_B):
        sl = slice(kv * HEAD_DIM, (kv + 1) * HEAD_DIM)
        ks = ks_ref[:, sl].astype(BF16)
        vs = vs_ref[:, sl].astype(BF16)
        kn = jnp.concatenate([kn_ref[:, sl], pad], axis=0).astype(BF16)
        vn = jnp.concatenate([vn_ref[:, sl], pad], axis=0).astype(BF16)
        q = jnp.concatenate(
            [q_refs[g][:, (kv * KV_REP + r) * HEAD_DIM:(kv * KV_REP + r + 1) * HEAD_DIM]
             for g in range(ng) for r in range(KV_REP)], axis=0).astype(BF16)
        dn = (((1,), (1,)), ((), ()))
        s1 = jnp.where(m_state, lax.dot_general(q, ks, dn, preferred_element_type=F32) * SCALE, NEG)
        s2 = jnp.where(m_new, lax.dot_general(q, kn, dn, preferred_element_type=F32) * SCALE, NEG)
        mx = jnp.maximum(jnp.max(s1, axis=-1, keepdims=True), jnp.max(s2, axis=-1, keepdims=True))
        p1 = jnp.exp(s1 - mx)
        p2 = jnp.exp(s2 - mx)
        den = jnp.sum(p1, axis=-1, keepdims=True) + jnp.sum(p2, axis=-1, keepdims=True)
        acc = (jnp.dot(p1.astype(BF16), vs, preferred_element_type=F32)
               + jnp.dot(p2.astype(BF16), vn, preferred_element_type=F32))
        gm = [mx[g * per_group:(g + 1) * per_group] for g in range(ng)]
        top = functools.reduce(jnp.maximum, gm)
        num = jnp.zeros((per_group, HEAD_DIM), F32)
        tot = jnp.zeros((per_group, 1), F32)
        for g in range(ng):
            e = jnp.exp(gm[g] - top)
            num += e * acc[g * per_group:(g + 1) * per_group]
            tot += e * den[g * per_group:(g + 1) * per_group]
        o = num * (1.0 / tot)
        for r in range(KV_REP):
            h = kv * KV_REP + r
            o_ref[:, h * HEAD_DIM:(h + 1) * HEAD_DIM] = o[r * dec_seq:(r + 1) * dec_seq]


def _dil_sample(q_parts, ks, vs, kn, vn, *, dec_batch, dec_seq):
    buf = ks.shape[1]
    kvw = N_KV_B * HEAD_DIM
    rowblk = pl.BlockSpec((dec_seq, D_MODEL), lambda b: (b, 0))
    return pl.pallas_call(
        functools.partial(_dil_sample_kernel, dec_seq=dec_seq, buf=buf),
        grid=(dec_batch,),
        in_specs=[rowblk] * 3
        + [pl.BlockSpec((None, buf, kvw), lambda b: (b, 0, 0))] * 2
        + [pl.BlockSpec((dec_seq, kvw), lambda b: (b, 0))] * 2,
        out_specs=rowblk,
        out_shape=jax.ShapeDtypeStruct((dec_batch * dec_seq, D_MODEL), F32),
        compiler_params=_params("parallel"),
    )(*q_parts, ks, vs, kn, vn)


def _rope_tables(pos):
    inv_freq = ROPE_THETA ** (-jnp.arange(0, HEAD_DIM, 2, dtype=F32) / HEAD_DIM)
    ang = pos.astype(F32)[:, None] * inv_freq[None, :]
    c, s = jnp.cos(ang), jnp.sin(ang)
    return jnp.concatenate([c, c], axis=1), jnp.concatenate([-s, s], axis=1)


def kernel(x_prompt, x_sample, cache_k_a, cache_v_a, state_k_b, state_v_b, page_table, g_attn_a, w_qkv_a, g_q_a, g_k_a, w_o_a, g_kv_b, w_kv_b, g_k_b, g_attn_b, w_q_b, g_q_b, w_o_b, g_ffn, w_gu, w_down):
    bp, sp, d = x_prompt.shape
    bs, ss, _ = x_sample.shape
    n_a = w_qkv_a.shape[0]
    n_b = w_q_b.shape[0]
    n_pool = cache_k_a.shape[1]
    n_pages = page_table.shape[1]
    buf = state_k_b.shape[1]
    kvw = N_KV_B * HEAD_DIM
    assert d == D_MODEL and n_a == 2 and PAST_LEN % MOBA_BLOCK == 0 and n_pages * PAGE_SIZE == PAST_LEN
    assert buf == max(w for w, _ in DIL_GROUPS) and sp >= buf and sp % MOBA_BLOCK == 0
    assert ss & (ss - 1) == 0 and all(dil & (dil - 1) == 0 for _, dil in DIL_GROUPS)

    w_qkv_a, w_o_a, w_kv_b, w_q_b, w_o_b, w_gu, w_down = (
        w.astype(BF16) for w in (w_qkv_a, w_o_a, w_kv_b, w_q_b, w_o_b, w_gu, w_down))
    cos_p, sin_p = _rope_tables(jnp.arange(sp))
    cos_s, sin_s = _rope_tables(PAST_LEN + jnp.arange(ss))
    cos_s, sin_s = jnp.tile(cos_s, (bs, 1)), jnp.tile(sin_s, (bs, 1))
    ones = jnp.ones((HEAD_DIM,), F32)

    mp, ms = bp * sp, bs * ss
    hp = x_prompt.reshape(mp, d)
    hs = x_sample.reshape(ms, d)
    cache_k4 = cache_k_a.reshape(n_a, n_pool, PAGE_SIZE, d)
    cache_v4 = cache_v_a.reshape(n_a, n_pool, PAGE_SIZE, d)
    page_flat = page_table.reshape(-1)

    tm_p, tm_s = 1024, ms
    kp = vp = ks = vs = None
    for l in range(n_a):
        gains = jnp.stack([g_q_a[l], g_k_a[l], ones]).reshape(3, 1, HEAD_DIM)
        outs = [(0, F32, True, None), (1, F32, True, l), (2, F32, False, l)]
        qp, kp, vp = _proj(hp, g_attn_a[l], w_qkv_a[l], gains, cos_p, sin_p, secw=d, outs=outs,
                           tm=tm_p, tn=512, pos_rows=sp, prev=None if l == 0 else {1: kp, 2: vp})
        qs, ks, vs = _proj(hs, g_attn_a[l], w_qkv_a[l], gains, cos_s, sin_s, secw=d, outs=outs,
                           tm=tm_s, tn=1024, pos_rows=ms, prev=None if l == 0 else {1: ks, 2: vs})
        op = _moba_prompt(qp, kp, vp, l, batch=bp, seq=sp)
        q3 = qs.reshape(bs, ss, d)
        idx = _moba_gate(q3, cache_k4, page_flat, l, dec_batch=bs, dec_seq=ss, n_pages=n_pages)
        idx_flat = idx[:, :, :N_HEADS].reshape(-1)
        os_ = _moba_sample(q3, ks.reshape(n_a, bs, ss, d), vs.reshape(n_a, bs, ss, d), cache_k4, cache_v4,
                           page_flat, idx_flat, l, dec_batch=bs, dec_seq=ss, n_pages=n_pages)
        hp = _oproj(op, w_o_a[l], hp, tm=512)
        hs = _oproj(os_.reshape(ms, d), w_o_a[l], hs, tm=ms)
        hp = _ffn(hp, g_ffn[l], w_gu[l], w_down[l], tm=512, tf=512)
        hs = _ffn(hs, g_ffn[l], w_gu[l], w_down[l], tm=ms, tf=512)

    gains_kv = jnp.stack([g_k_b, ones]).reshape(2, 1, HEAD_DIM)
    outs_kv = [(0, F32, True, None), (0, BF16, True, None), (1, F32, False, None), (1, BF16, False, None)]
    kbp, kbp16, vbp, vbp16 = _proj(hp, g_kv_b, w_kv_b, gains_kv, cos_p, sin_p, secw=kvw, outs=outs_kv,
                                   tm=tm_p, tn=kvw, pos_rows=sp)
    kbs, _, vbs, _ = _proj(hs, g_kv_b, w_kv_b, gains_kv, cos_s, sin_s, secw=kvw, outs=outs_kv,
                           tm=tm_s, tn=kvw, pos_rows=ms)
    state_k3 = state_k_b.reshape(bs, buf, kvw)
    state_v3 = state_v_b.reshape(bs, buf, kvw)

    for j in range(n_b):
        l = n_a + j
        gains = jnp.stack([g_q_b[j]] * 3).reshape(3, 1, HEAD_DIM)
        q_parts = _proj(hp, g_attn_b[j], w_q_b[j], gains, cos_p, sin_p, secw=d,
                        outs=[(g, BF16, True, None) for g in range(3)], tm=tm_p, tn=512, pos_rows=sp)
        qs_parts = _proj(hs, g_attn_b[j], w_q_b[j], gains, cos_s, sin_s, secw=d,
                         outs=[(g, F32, True, None) for g in range(3)], tm=tm_s, tn=1024, pos_rows=ms)
        o_parts, lse_parts = [], []
        for g, (_, dil) in enumerate(DIL_GROUPS):
            o_g, lse_g = _dil_prompt_group(q_parts[g], kbp16, vbp16, dil, batch=bp, seq=sp)
            o_parts.append(o_g)
            lse_parts.append(lse_g)
        hp = _merge_oproj(o_parts, lse_parts, w_o_b[j], hp, tm=256)
        os_ = _dil_sample(qs_parts, state_k3, state_v3, kbs, vbs, dec_batch=bs, dec_seq=ss)
        hs = _oproj(os_, w_o_b[j], hs, tm=ms)
        hp = _ffn(hp, g_ffn[l], w_gu[l], w_down[l], tm=512, tf=512)
        hs = _ffn(hs, g_ffn[l], w_gu[l], w_down[l], tm=ms, tf=512)

    hd = (N_HEADS, HEAD_DIM)
    k_all_s = jnp.concatenate([state_k3, kbs.reshape(bs, ss, kvw)], axis=1)[:, ss:]
    v_all_s = jnp.concatenate([state_v3, vbs.reshape(bs, ss, kvw)], axis=1)[:, ss:]
    return (hp.reshape(bp, sp, d), hs.reshape(bs, ss, d),
            kp.reshape(n_a, bp, sp, *hd), vp.reshape(n_a, bp, sp, *hd),
            ks.reshape(n_a, bs, ss, *hd), vs.reshape(n_a, bs, ss, *hd),
            kbp.reshape(bp, sp, N_KV_B, HEAD_DIM)[:, sp - buf:], vbp.reshape(bp, sp, N_KV_B, HEAD_DIM)[:, sp - buf:],
            k_all_s.reshape(bs, buf, N_KV_B, HEAD_DIM), v_all_s.reshape(bs, buf, N_KV_B, HEAD_DIM))
```

```python
import functools

import jax
import jax.numpy as jnp
from jax import lax
from jax.experimental import pallas as pl
from jax.experimental.pallas import tpu as pltpu

F32 = jnp.float32
BF16 = jnp.bfloat16

D_MODEL = 2048
HEAD_DIM = 128
N_HEADS = D_MODEL // HEAD_DIM
PAST_LEN = 16384
PAGE_SIZE = 128
MOBA_BLOCK = 256
MOBA_TOPK = 3
DIL_GROUPS = ((128, 1), (512, 4), (2048, 16))
N_KV_B = 4
KV_REP = N_HEADS // N_KV_B
ROPE_THETA = 10000.0
EPS = 1e-6
NEG = -1e30
SCALE = HEAD_DIM ** -0.5

VMEM_LIMIT_V7X = 56 * 1024 * 1024

_NT = (((1,), (1,)), ((), ()))


def _params(*sem):
    return pltpu.CompilerParams(dimension_semantics=sem, vmem_limit_bytes=VMEM_LIMIT_V7X)


def _rms_scale(x):
    return lax.rsqrt(jnp.mean(x * x, axis=-1, keepdims=True) + EPS)


def _head_norm_rope(acc, gain, cos, sin):
    outs = []
    for h in range(acc.shape[1] // HEAD_DIM):
        a = acc[:, h * HEAD_DIM:(h + 1) * HEAD_DIM]
        y = a * _rms_scale(a) * gain
        outs.append(y * cos + pltpu.roll(y, HEAD_DIM // 2, axis=1) * sin)
    return outs[0] if len(outs) == 1 else jnp.concatenate(outs, axis=1)


def _proj_kernel(x_ref, g_ref, w_ref, gain_ref, cos_ref, sin_ref, *rest, nj, outs, n_alias):
    out_refs = rest[n_alias:n_alias + len(outs)]
    xn_ref = rest[n_alias + len(outs)]
    c = pl.program_id(1)

    @pl.when(c == 0)
    def _():
        x = x_ref[...]
        xn_ref[...] = (x * _rms_scale(x) * g_ref[...]).astype(BF16)

    acc = jnp.dot(xn_ref[...], w_ref[...], preferred_element_type=F32)
    sections = sorted({sec for sec, _, _, _ in outs})
    for sec in sections:
        rope = [r for s, _, r, _ in outs if s == sec][0]

        @pl.when(c // nj == sec)
        def _(sec=sec, rope=rope):
            y = _head_norm_rope(acc, gain_ref[...], cos_ref[...], sin_ref[...]) if rope else acc
            for o_ref, (s, dt, _, _) in zip(out_refs, outs):
                if s == sec:
                    o_ref[...] = y.astype(dt)


def _proj(x, g, w, gains, cos, sin, *, secw, outs, tm, tn, pos_rows, prev=None):
    m, d = x.shape
    nsec = w.shape[1] // secw
    nj = secw // tn
    ni = m // tm
    npos = pos_rows // tm
    prev = prev or {}

    def out_map(sec, layer):
        def f(i, c):
            jj = jnp.clip(c - sec * nj, 0, nj - 1)
            return (i, jj) if layer is None else (layer, i, jj)
        return f

    out_shape, out_specs, aliases, alias_inputs = [], [], {}, []
    for k, (sec, dt, _, layer) in enumerate(outs):
        if layer is None:
            out_shape.append(jax.ShapeDtypeStruct((m, secw), dt))
            out_specs.append(pl.BlockSpec((tm, tn), out_map(sec, None)))
        else:
            out_shape.append(jax.ShapeDtypeStruct((2, m, secw), dt))
            out_specs.append(pl.BlockSpec((None, tm, tn), out_map(sec, layer)))
            if k in prev:
                aliases[6 + len(alias_inputs)] = k
                alias_inputs.append(prev[k])
    in_specs = [
        pl.BlockSpec((tm, d), lambda i, c: (i, 0)),
        pl.BlockSpec((1, d), lambda i, c: (0, 0)),
        pl.BlockSpec((d, tn), lambda i, c: (0, c)),
        pl.BlockSpec((None, 1, HEAD_DIM), lambda i, c: (c // nj, 0, 0)),
        pl.BlockSpec((tm, HEAD_DIM), lambda i, c: (i % npos, 0)),
        pl.BlockSpec((tm, HEAD_DIM), lambda i, c: (i % npos, 0)),
    ] + [pl.BlockSpec(memory_space=pl.ANY)] * len(alias_inputs)
    return pl.pallas_call(
        functools.partial(_proj_kernel, nj=nj, outs=outs, n_alias=len(alias_inputs)),
        grid=(ni, nsec * nj),
        in_specs=in_specs,
        out_specs=out_specs,
        out_shape=out_shape,
        scratch_shapes=[pltpu.VMEM((tm, d), BF16)],
        input_output_aliases=aliases,
        compiler_params=_params("parallel", "arbitrary"),
        name="norm_proj",
    )(x, g.reshape(1, d), w, gains, cos, sin, *alias_inputs)


def _oproj_kernel(x_ref, w_ref, r_ref, o_ref):
    o_ref[...] = r_ref[...] + jnp.dot(x_ref[...].astype(BF16), w_ref[...], preferred_element_type=F32)


def _oproj(x, w, res, *, tm):
    m, d = x.shape
    n = w.shape[1]
    return pl.pallas_call(
        _oproj_kernel,
        grid=(m // tm,),
        in_specs=[pl.BlockSpec((tm, d), lambda i: (i, 0)),
                  pl.BlockSpec((d, n), lambda i: (0, 0)),
                  pl.BlockSpec((tm, n), lambda i: (i, 0))],
        out_specs=pl.BlockSpec((tm, n), lambda i: (i, 0)),
        out_shape=jax.ShapeDtypeStruct((m, n), F32),
        compiler_params=_params("parallel"),
        name="out_proj",
    )(x, w, res)


def _ffn_kernel(x_ref, g_ref, wg_ref, wu_ref, wd_ref, o_ref, xn_ref, acc_ref):
    j = pl.program_id(1)

    @pl.when(j == 0)
    def _():
        x = x_ref[...]
        xn_ref[...] = (x * _rms_scale(x) * g_ref[...]).astype(BF16)
        acc_ref[...] = jnp.zeros_like(acc_ref)

    xn = xn_ref[...]
    gate = jnp.dot(xn, wg_ref[...], preferred_element_type=F32)
    up = jnp.dot(xn, wu_ref[...], preferred_element_type=F32)
    act = (gate * jax.nn.sigmoid(gate) * up).astype(BF16)
    acc_ref[...] += jnp.dot(act, wd_ref[...], preferred_element_type=F32)

    @pl.when(j == pl.num_programs(1) - 1)
    def _():
        o_ref[...] = x_ref[...] + acc_ref[...]


def _ffn(x, g, w_gu, w_down, *, tm, tf):
    m, d = x.shape
    dff = w_down.shape[0]
    nf = dff // tf
    return pl.pallas_call(
        _ffn_kernel,
        grid=(m // tm, nf),
        in_specs=[pl.BlockSpec((tm, d), lambda i, j: (i, 0)),
                  pl.BlockSpec((1, d), lambda i, j: (0, 0)),
                  pl.BlockSpec((d, tf), lambda i, j: (0, j)),
                  pl.BlockSpec((d, tf), lambda i, j: (0, j + nf)),
                  pl.BlockSpec((tf, d), lambda i, j: (j, 0))],
        out_specs=pl.BlockSpec((tm, d), lambda i, j: (i, 0)),
        out_shape=jax.ShapeDtypeStruct((m, d), F32),
        scratch_shapes=[pltpu.VMEM((tm, d), BF16), pltpu.VMEM((tm, d), F32)],
        compiler_params=_params("parallel", "arbitrary"),
        name="swiglu_ffn",
    )(x, g.reshape(1, d), w_gu, w_gu, w_down)


def _moba_prompt_kernel(q_ref, k_ref, v_ref, o_ref, *, seq):
    nb = seq // MOBA_BLOCK
    kb = k_ref[...].astype(BF16)
    vb = v_ref[...].astype(BF16)
    kmean = [jnp.mean(k_ref[j * MOBA_BLOCK:(j + 1) * MOBA_BLOCK, :], axis=0, keepdims=True)
             for j in range(nb - 1)]
    row = lax.broadcasted_iota(jnp.int32, (MOBA_BLOCK, MOBA_BLOCK), 0)
    col = lax.broadcasted_iota(jnp.int32, (MOBA_BLOCK, MOBA_BLOCK), 1)
    causal = col <= row
    for i in range(nb):
        q = q_ref[i * MOBA_BLOCK:(i + 1) * MOBA_BLOCK, :]
        n = (i + 1) * MOBA_BLOCK
        s = lax.dot_general(q.astype(BF16), kb[:n], _NT, preferred_element_type=F32) * SCALE
        tiles = []
        if i > MOBA_TOPK:
            gate = [jnp.sum(q * kmean[j], axis=-1, keepdims=True) for j in range(i)]
            for j in range(i):
                rank = jnp.zeros((MOBA_BLOCK, 1), F32)
                for j2 in range(i):
                    if j2 < j:
                        rank += jnp.where(gate[j2] >= gate[j], 1.0, 0.0)
                    elif j2 > j:
                        rank += jnp.where(gate[j2] > gate[j], 1.0, 0.0)
                tiles.append(jnp.where(rank < MOBA_TOPK, s[:, j * MOBA_BLOCK:(j + 1) * MOBA_BLOCK], NEG))
        else:
            tiles = [s[:, j * MOBA_BLOCK:(j + 1) * MOBA_BLOCK] for j in range(i)]
        tiles.append(jnp.where(causal, s[:, i * MOBA_BLOCK:], NEG))
        sm = tiles[0] if len(tiles) == 1 else jnp.concatenate(tiles, axis=1)
        mx = jnp.max(sm, axis=-1, keepdims=True)
        p = jnp.exp(sm - mx)
        den = jnp.sum(p, axis=-1, keepdims=True)
        o = jnp.dot(p.astype(BF16), vb[:n], preferred_element_type=F32) * (1.0 / den)
        o_ref[i * MOBA_BLOCK:(i + 1) * MOBA_BLOCK, :] = o.astype(o_ref.dtype)


def _moba_prompt(q, kbuf, vbuf, layer, *, batch, seq):
    m = q.shape[0]
    return pl.pallas_call(
        functools.partial(_moba_prompt_kernel, seq=seq),
        grid=(batch, N_HEADS),
        in_specs=[pl.BlockSpec((seq, HEAD_DIM), lambda b, h: (b, h)),
                  pl.BlockSpec((None, seq, HEAD_DIM), lambda b, h: (layer, b, h)),
                  pl.BlockSpec((None, seq, HEAD_DIM), lambda b, h: (layer, b, h))],
        out_specs=pl.BlockSpec((seq, HEAD_DIM), lambda b, h: (b, h)),
        out_shape=jax.ShapeDtypeStruct((m, D_MODEL), BF16),
        compiler_params=_params("parallel", "parallel"),
        name="moba_prompt",
    )(q, kbuf, vbuf)


PAGES_PER_BLOCK = MOBA_BLOCK // PAGE_SIZE
N_PAST_BLOCKS = PAST_LEN // MOBA_BLOCK
SAMPLE_BLOCKS_PER_STEP = 2
SAMPLE_PAGES_PER_STEP = SAMPLE_BLOCKS_PER_STEP * PAGES_PER_BLOCK
BLOCK_ROWS = MOBA_BLOCK * N_HEADS
LANES = 128


def _moba_sample_kernel(pt_ref, qt_ref, qf_ref, kn_ref, vn_ref, *rest, dec_seq):
    k_pages = rest[:SAMPLE_PAGES_PER_STEP]
    v_pages = rest[SAMPLE_PAGES_PER_STEP:2 * SAMPLE_PAGES_PER_STEP]
    o_ref, bias_ref, acc_ref, m_ref, l_ref, km_ref = rest[2 * SAMPLE_PAGES_PER_STEP:]
    step = pl.program_id(1)
    nq = dec_seq * N_HEADS
    lane = lax.broadcasted_iota(jnp.int32, (nq, LANES), 1)

    @pl.when(step == 0)
    def _():
        r = lax.broadcasted_iota(jnp.int32, (nq, BLOCK_ROWS), 0)
        c = lax.broadcasted_iota(jnp.int32, (nq, BLOCK_ROWS), 1)
        bias_ref[...] = jnp.where((r & (N_HEADS - 1)) == (c & (N_HEADS - 1)), 0.0, NEG)
        m_ref[...] = jnp.full((nq, LANES), NEG, F32)
        l_ref[...] = jnp.zeros((nq, LANES), F32)

    qt = qt_ref[...]
    for p in range(SAMPLE_BLOCKS_PER_STEP):
        j = step * SAMPLE_BLOCKS_PER_STEP + p
        kf = [k_pages[p * PAGES_PER_BLOCK + u][...] for u in range(PAGES_PER_BLOCK)]
        vf = [v_pages[p * PAGES_PER_BLOCK + u][...] for u in range(PAGES_PER_BLOCK)]
        km_ref[j] = sum(jnp.sum(x, axis=0) for x in kf) * (1.0 / MOBA_BLOCK)
        k2 = jnp.concatenate([x.reshape(PAGE_SIZE * N_HEADS, HEAD_DIM).astype(BF16) for x in kf], axis=0)
        v2 = jnp.concatenate([x.reshape(PAGE_SIZE * N_HEADS, HEAD_DIM).astype(BF16) for x in vf], axis=0)
        s = lax.dot_general(qt, k2, _NT, preferred_element_type=F32) * SCALE + bias_ref[...]
        mx = jnp.max(s, axis=-1, keepdims=True)
        pe = jnp.exp(s - mx)
        acc_ref[j] = jnp.dot(pe.astype(BF16), v2, preferred_element_type=F32)
        m_ref[...] = jnp.where(lane == j, mx, m_ref[...])
        l_ref[...] = jnp.where(lane == j, jnp.sum(pe, axis=-1, keepdims=True), l_ref[...])

    @pl.when(step == pl.num_programs(1) - 1)
    def _():
        qf = qf_ref[...]

        def gate_body(j, gate):
            kmt = jnp.concatenate([km_ref[j]] * dec_seq, axis=0)
            return jnp.where(lane == j, jnp.sum(qf * kmt, axis=-1, keepdims=True), gate)

        gate = lax.fori_loop(0, N_PAST_BLOCKS, gate_body, jnp.full((nq, LANES), -jnp.inf, F32))
        lane_f = lane.astype(F32)
        sel = jnp.zeros((nq, LANES), F32)
        for _ in range(MOBA_TOPK):
            top = jnp.max(gate, axis=-1, keepdims=True)
            pick = jnp.min(jnp.where(gate == top, lane_f, float(LANES)), axis=-1, keepdims=True)
            hit = lane_f == pick
            sel = jnp.where(hit, 1.0, sel)
            gate = jnp.where(hit, -jnp.inf, gate)

        r = lax.broadcasted_iota(jnp.int32, (nq, nq), 0)
        c = lax.broadcasted_iota(jnp.int32, (nq, nq), 1)
        own_ok = ((r & (N_HEADS - 1)) == (c & (N_HEADS - 1))) & (c <= r)
        s_own = jnp.where(own_ok, lax.dot_general(qt, kn_ref[...].astype(BF16), _NT,
                                                  preferred_element_type=F32) * SCALE, NEG)
        m_blk = m_ref[...]
        m_tot = jnp.maximum(jnp.max(jnp.where(sel > 0, m_blk, NEG), axis=-1, keepdims=True),
                            jnp.max(s_own, axis=-1, keepdims=True))
        p_own = jnp.exp(s_own - m_tot)
        w = jnp.where(sel > 0, jnp.exp(jnp.minimum(m_blk - m_tot, 0.0)), 0.0)
        den = jnp.sum(p_own, axis=-1, keepdims=True) + jnp.sum(w * l_ref[...], axis=-1, keepdims=True)
        num = jnp.dot(p_own.astype(BF16), vn_ref[...].astype(BF16), preferred_element_type=F32)

        def merge_body(j, num):
            wj = jnp.sum(jnp.where(lane == j, w, 0.0), axis=-1, keepdims=True)
            return num + wj * acc_ref[j]

        num = lax.fori_loop(0, N_PAST_BLOCKS, merge_body, num)
        o_ref[...] = num * (1.0 / den)


def _moba_sample(qs, knew, vnew, cache_k, cache_v, page_flat, layer, *, dec_batch, dec_seq, n_pages):
    nq = dec_seq * N_HEADS
    rows = lambda a: a.reshape(dec_batch, nq, HEAD_DIM)
    qf = rows(qs)

    def page_map(u):
        return lambda b, s, pt: (layer, pt[b * n_pages + s * SAMPLE_PAGES_PER_STEP + u], 0, 0, 0)

    per_b = pl.BlockSpec((None, nq, HEAD_DIM), lambda b, s, pt: (b, 0, 0))
    pages = [pl.BlockSpec((None, None, PAGE_SIZE, N_HEADS, HEAD_DIM), page_map(u))
             for u in range(SAMPLE_PAGES_PER_STEP)]
    grid_spec = pltpu.PrefetchScalarGridSpec(
        num_scalar_prefetch=1,
        grid=(dec_batch, N_PAST_BLOCKS // SAMPLE_BLOCKS_PER_STEP),
        in_specs=[per_b] * 4 + pages * 2,
        out_specs=per_b,
        scratch_shapes=[pltpu.VMEM((nq, BLOCK_ROWS), F32),
                        pltpu.VMEM((N_PAST_BLOCKS, nq, HEAD_DIM), F32),
                        pltpu.VMEM((nq, LANES), F32),
                        pltpu.VMEM((nq, LANES), F32),
                        pltpu.VMEM((N_PAST_BLOCKS, N_HEADS, HEAD_DIM), F32)],
    )
    o = pl.pallas_call(
        functools.partial(_moba_sample_kernel, dec_seq=dec_seq),
        grid_spec=grid_spec,
        out_shape=jax.ShapeDtypeStruct((dec_batch, nq, HEAD_DIM), F32),
        compiler_params=_params("parallel", "arbitrary"),
        name="moba_sample",
    )(page_flat, qf.astype(BF16), qf, rows(knew), rows(vnew),
      *([cache_k] * SAMPLE_PAGES_PER_STEP), *([cache_v] * SAMPLE_PAGES_PER_STEP))
    return o.reshape(dec_batch * dec_seq, D_MODEL)


DIL_TILE = 128


def _dil_prompt_kernel(*refs, seq):
    ng = len(DIL_GROUPS)
    q_refs = [refs[g * KV_REP:(g + 1) * KV_REP] for g in range(ng)]
    k_ref, v_ref, o_ref, acc_ref, m_ref, l_ref = refs[ng * KV_REP:]
    stacked = KV_REP * DIL_TILE
    a1 = lax.broadcasted_iota(jnp.int32, (stacked, DIL_TILE), 0) & (DIL_TILE - 1)
    c1 = lax.broadcasted_iota(jnp.int32, (stacked, DIL_TILE), 1)
    causal = c1 <= a1
    a2 = lax.broadcasted_iota(jnp.int32, (stacked, 2 * DIL_TILE), 0) & (DIL_TILE - 1)
    c2 = lax.broadcasted_iota(jnp.int32, (stacked, 2 * DIL_TILE), 1)
    band = (c2 >= a2) & (c2 <= a2 + DIL_TILE)
    wide = (DIL_TILE, HEAD_DIM)

    def rows(start, size, dil):
        return pl.ds(start, size) if dil == 1 else pl.ds(start, size, stride=dil)

    def tile(g, dil, q_start, k_start, nk):
        qrows = rows(q_start, DIL_TILE, dil)
        qs = jnp.concatenate([q_refs[g][r][qrows, :].astype(BF16) for r in range(KV_REP)], axis=0)
        kw = k_ref[rows(k_start, nk, dil), :].astype(BF16)
        vw = v_ref[rows(k_start, nk, dil), :].astype(BF16)
        s = lax.dot_general(qs, kw, _NT, preferred_element_type=F32) * SCALE
        s = jnp.where(causal if nk == DIL_TILE else band, s, NEG)
        mx = jnp.max(s, axis=-1, keepdims=True)
        p = jnp.exp(s - mx)
        den = jnp.sum(p, axis=-1, keepdims=True)
        num = jnp.dot(p.astype(BF16), vw, preferred_element_type=F32)
        for r in range(KV_REP):
            sl = slice(r * DIL_TILE, (r + 1) * DIL_TILE)
            m_new = jnp.broadcast_to(mx[sl], wide)
            l_new = jnp.broadcast_to(den[sl], wide)
            a_new = num[sl]
            if g > 0:
                m_old = m_ref[r, qrows, :]
                m_top = jnp.maximum(m_old, m_new)
                w_old, w_new = jnp.exp(m_old - m_top), jnp.exp(m_new - m_top)
                a_new = acc_ref[r, qrows, :] * w_old + a_new * w_new
                l_new = l_ref[r, qrows, :] * w_old + l_new * w_new
                m_new = m_top
            acc_ref[r, qrows, :] = a_new
            m_ref[r, qrows, :] = m_new
            l_ref[r, qrows, :] = l_new

    for g, (_, dil) in enumerate(DIL_GROUPS):
        nt = seq // dil // DIL_TILE
        for r in range(dil):
            tile(g, dil, r, r, DIL_TILE)
        if nt > 1:
            def body(t, carry, g=g, dil=dil):
                base = pl.multiple_of((t - 1) * (DIL_TILE * dil), DIL_TILE)
                for r in range(dil):
                    tile(g, dil, base + (DIL_TILE * dil + r), base + r, 2 * DIL_TILE)
                return carry

            lax.fori_loop(1, nt, body, 0, unroll=3 if dil == 1 and (nt - 1) % 3 == 0 else 1)

    def finish(t, carry):
        start = pl.multiple_of(t * DIL_TILE, DIL_TILE)
        for r in range(KV_REP):
            inv = 1.0 / l_ref[r, pl.ds(start, DIL_TILE), :]
            o_ref[pl.ds(start, DIL_TILE), r * HEAD_DIM:(r + 1) * HEAD_DIM] = (
                acc_ref[r, pl.ds(start, DIL_TILE), :] * inv).astype(o_ref.dtype)
        return carry

    lax.fori_loop(0, seq // DIL_TILE, finish, 0)


def _dil_prompt(q_parts, k, v, *, batch, seq):
    m = q_parts[0].shape[0]

    def head_blk(r):
        return pl.BlockSpec((seq, HEAD_DIM), lambda b, g: (b, g * KV_REP + r))

    kv_blk = pl.BlockSpec((seq, HEAD_DIM), lambda b, g: (b, g))
    q_in = [q for q in q_parts for _ in range(KV_REP)]
    return pl.pallas_call(
        functools.partial(_dil_prompt_kernel, seq=seq),
        grid=(batch, N_KV_B),
        in_specs=[head_blk(r) for _ in q_parts for r in range(KV_REP)] + [kv_blk] * 2,
        out_specs=pl.BlockSpec((seq, KV_REP * HEAD_DIM), lambda b, g: (b, g)),
        out_shape=jax.ShapeDtypeStruct((m, D_MODEL), BF16),
        scratch_shapes=[pltpu.VMEM((KV_REP, seq, HEAD_DIM), F32)] * 3,
        compiler_params=_params("parallel", "parallel"),
        name="dilated_prompt",
    )(*q_in, k, v)


def _dil_sample_kernel(q0_ref, q1_ref, q2_ref, ks_ref, vs_ref, kn_ref, vn_ref, o_ref, *, dec_seq, buf):
    q_refs = (q0_ref, q1_ref, q2_ref)
    ng = len(DIL_GROUPS)
    per_group = KV_REP * dec_seq
    nrow = ng * per_group
    pad = jnp.zeros((HEAD_DIM - dec_seq, HEAD_DIM), F32)

    def masks(ncol, is_state):
        rowi = lax.broadcasted_iota(jnp.int32, (nrow, ncol), 0)
        col = lax.broadcasted_iota(jnp.int32, (nrow, ncol), 1)
        t = rowi & (dec_seq - 1)
        dil = jnp.full((nrow, ncol), DIL_GROUPS[-1][1], jnp.int32)
        win = jnp.full((nrow, ncol), DIL_GROUPS[-1][0], jnp.int32)
        for g in range(ng - 2, -1, -1):
            in_g = rowi < (g + 1) * per_group
            dil = jnp.where(in_g, DIL_GROUPS[g][1], dil)
            win = jnp.where(in_g, DIL_GROUPS[g][0], win)
        dist = (buf + t - col) if is_state else (t - col)
        ok = (dist >= 0) & ((dist & (dil - 1)) == 0) & (dist <= win)
        if not is_state:
            ok = ok & (col < dec_seq)
        return ok

    m_state = masks(buf, True)
    m_new = masks(HEAD_DIM, False)
    for kv in range(N_KV_B):
        sl = slice(kv * HEAD_DIM, (kv + 1) * HEAD_DIM)
        ks = ks_ref[:, sl].astype(BF16)
        vs = vs_ref[:, sl].astype(BF16)
        kn = jnp.concatenate([kn_ref[:, sl], pad], axis=0).astype(BF16)
        vn = jnp.concatenate([vn_ref[:, sl], pad], axis=0).astype(BF16)
        q = jnp.concatenate(
            [q_refs[g][:, (kv * KV_REP + r) * HEAD_DIM:(kv * KV_REP + r + 1) * HEAD_DIM]
             for g in range(ng) for r in range(KV_REP)], axis=0).astype(BF16)
        s1 = jnp.where(m_state, lax.dot_general(q, ks, _NT, preferred_element_type=F32) * SCALE, NEG)
        s2 = jnp.where(m_new, lax.dot_general(q, kn, _NT, preferred_element_type=F32) * SCALE, NEG)
        mx = jnp.maximum(jnp.max(s1, axis=-1, keepdims=True), jnp.max(s2, axis=-1, keepdims=True))
        p1 = jnp.exp(s1 - mx)
        p2 = jnp.exp(s2 - mx)
        den = jnp.sum(p1, axis=-1, keepdims=True) + jnp.sum(p2, axis=-1, keepdims=True)
        acc = (jnp.dot(p1.astype(BF16), vs, preferred_element_type=F32)
               + jnp.dot(p2.astype(BF16), vn, preferred_element_type=F32))
        gm = [mx[g * per_group:(g + 1) * per_group] for g in range(ng)]
        top = functools.reduce(jnp.maximum, gm)
        num = jnp.zeros((per_group, HEAD_DIM), F32)
        tot = jnp.zeros((per_group, 1), F32)
        for g in range(ng):
            e = jnp.exp(gm[g] - top)
            num += e * acc[g * per_group:(g + 1) * per_group]
            tot += e * den[g * per_group:(g + 1) * per_group]
        o = num * (1.0 / tot)
        for r in range(KV_REP):
            h = kv * KV_REP + r
            o_ref[:, h * HEAD_DIM:(h + 1) * HEAD_DIM] = o[r * dec_seq:(r + 1) * dec_seq]


def _dil_sample(q_parts, ks, vs, kn, vn, *, dec_batch, dec_seq):
    buf = ks.shape[1]
    kvw = N_KV_B * HEAD_DIM
    rowblk = pl.BlockSpec((dec_seq, D_MODEL), lambda b: (b, 0))
    return pl.pallas_call(
        functools.partial(_dil_sample_kernel, dec_seq=dec_seq, buf=buf),
        grid=(dec_batch,),
        in_specs=[rowblk] * 3
        + [pl.BlockSpec((None, buf, kvw), lambda b: (b, 0, 0))] * 2
        + [pl.BlockSpec((dec_seq, kvw), lambda b: (b, 0))] * 2,
        out_specs=rowblk,
        out_shape=jax.ShapeDtypeStruct((dec_batch * dec_seq, D_MODEL), F32),
        compiler_params=_params("parallel"),
        name="dilated_sample",
    )(*q_parts, ks, vs, kn, vn)


def _rope_tables(pos):
    inv_freq = ROPE_THETA ** (-jnp.arange(0, HEAD_DIM, 2, dtype=F32) / HEAD_DIM)
    ang = pos.astype(F32)[:, None] * inv_freq[None, :]
    c, s = jnp.cos(ang), jnp.sin(ang)
    return jnp.concatenate([c, c], axis=1), jnp.concatenate([-s, s], axis=1)


def kernel(x_prompt, x_sample, cache_k_a, cache_v_a, state_k_b, state_v_b, page_table, g_attn_a, w_qkv_a, g_q_a, g_k_a, w_o_a, g_kv_b, w_kv_b, g_k_b, g_attn_b, w_q_b, g_q_b, w_o_b, g_ffn, w_gu, w_down):
    bp, sp, d = x_prompt.shape
    bs, ss, _ = x_sample.shape
    n_a = w_qkv_a.shape[0]
    n_b = w_q_b.shape[0]
    n_pages = page_table.shape[1]
    buf = state_k_b.shape[1]
    kvw = N_KV_B * HEAD_DIM
    assert d == D_MODEL and n_a == 2 and PAST_LEN % MOBA_BLOCK == 0 and n_pages * PAGE_SIZE == PAST_LEN
    assert cache_k_a.shape[2:] == (PAGE_SIZE, N_HEADS, HEAD_DIM) and ss * N_HEADS == LANES
    assert buf == max(w for w, _ in DIL_GROUPS) and sp >= buf and sp % MOBA_BLOCK == 0
    assert ss & (ss - 1) == 0 and all(dil & (dil - 1) == 0 for _, dil in DIL_GROUPS)

    w_qkv_a, w_o_a, w_kv_b, w_q_b, w_o_b, w_gu, w_down = (
        w.astype(BF16) for w in (w_qkv_a, w_o_a, w_kv_b, w_q_b, w_o_b, w_gu, w_down))
    cos_p, sin_p = _rope_tables(jnp.arange(sp))
    cos_s, sin_s = _rope_tables(PAST_LEN + jnp.arange(ss))
    cos_s, sin_s = jnp.tile(cos_s, (bs, 1)), jnp.tile(sin_s, (bs, 1))
    ones = jnp.ones((HEAD_DIM,), F32)

    mp, ms = bp * sp, bs * ss
    hp = x_prompt.reshape(mp, d)
    hs = x_sample.reshape(ms, d)
    page_flat = page_table.reshape(-1)

    tm_p, tm_s = 1024, ms
    kp = vp = ks = vs = None
    for l in range(n_a):
        gains = jnp.stack([g_q_a[l], g_k_a[l], ones]).reshape(3, 1, HEAD_DIM)
        outs = [(0, F32, True, None), (1, F32, True, l), (2, F32, False, l)]
        qp, kp, vp = _proj(hp, g_attn_a[l], w_qkv_a[l], gains, cos_p, sin_p, secw=d, outs=outs,
                           tm=tm_p, tn=512, pos_rows=sp, prev=None if l == 0 else {1: kp, 2: vp})
        qs, ks, vs = _proj(hs, g_attn_a[l], w_qkv_a[l], gains, cos_s, sin_s, secw=d, outs=outs,
                           tm=tm_s, tn=1024, pos_rows=ms, prev=None if l == 0 else {1: ks, 2: vs})
        op = _moba_prompt(qp, kp, vp, l, batch=bp, seq=sp)
        os_ = _moba_sample(qs, ks[l], vs[l], cache_k_a, cache_v_a, page_flat, l,
                           dec_batch=bs, dec_seq=ss, n_pages=n_pages)
        hp = _oproj(op, w_o_a[l], hp, tm=512)
        hs = _oproj(os_, w_o_a[l], hs, tm=ms)
        hp = _ffn(hp, g_ffn[l], w_gu[l], w_down[l], tm=512, tf=512)
        hs = _ffn(hs, g_ffn[l], w_gu[l], w_down[l], tm=ms, tf=512)

    gains_kv = jnp.stack([g_k_b, ones]).reshape(2, 1, HEAD_DIM)
    outs_kv = [(0, F32, True, None), (1, F32, False, None)]
    kbp, vbp = _proj(hp, g_kv_b, w_kv_b, gains_kv, cos_p, sin_p, secw=kvw, outs=outs_kv,
                     tm=tm_p, tn=kvw, pos_rows=sp)
    kbs, vbs = _proj(hs, g_kv_b, w_kv_b, gains_kv, cos_s, sin_s, secw=kvw, outs=outs_kv,
                     tm=tm_s, tn=kvw, pos_rows=ms)
    state_k3 = state_k_b.reshape(bs, buf, kvw)
    state_v3 = state_v_b.reshape(bs, buf, kvw)

    for j in range(n_b):
        l = n_a + j
        gains = jnp.stack([g_q_b[j]] * 3).reshape(3, 1, HEAD_DIM)
        q_parts = _proj(hp, g_attn_b[j], w_q_b[j], gains, cos_p, sin_p, secw=d,
                        outs=[(g, BF16 if dil == 1 else F32, True, None) for g, (_, dil) in enumerate(DIL_GROUPS)],
                        tm=tm_p, tn=512, pos_rows=sp)
        qs_parts = _proj(hs, g_attn_b[j], w_q_b[j], gains, cos_s, sin_s, secw=d,
                         outs=[(g, F32, True, None) for g in range(3)], tm=tm_s, tn=1024, pos_rows=ms)
        op = _dil_prompt(q_parts, kbp, vbp, batch=bp, seq=sp)
        hp = _oproj(op, w_o_b[j], hp, tm=512)
        os_ = _dil_sample(qs_parts, state_k3, state_v3, kbs, vbs, dec_batch=bs, dec_seq=ss)
        hs = _oproj(os_, w_o_b[j], hs, tm=ms)
        hp = _ffn(hp, g_ffn[l], w_gu[l], w_down[l], tm=512, tf=512)
        hs = _ffn(hs, g_ffn[l], w_gu[l], w_down[l], tm=ms, tf=512)

    hd = (N_HEADS, HEAD_DIM)
    k_all_s = jnp.concatenate([state_k3, kbs.reshape(bs, ss, kvw)], axis=1)[:, ss:]
    v_all_s = jnp.concatenate([state_v3, vbs.reshape(bs, ss, kvw)], axis=1)[:, ss:]
    return (hp.reshape(bp, sp, d), hs.reshape(bs, ss, d),
            kp.reshape(n_a, bp, sp, *hd), vp.reshape(n_a, bp, sp, *hd),
            ks.reshape(n_a, bs, ss, *hd), vs.reshape(n_a, bs, ss, *hd),
            kbp.reshape(bp, sp, N_KV_B, HEAD_DIM)[:, sp - buf:], vbp.reshape(bp, sp, N_KV_B, HEAD_DIM)[:, sp - buf:],
            k_all_s.reshape(bs, buf, N_KV_B, HEAD_DIM), v_all_s.reshape(bs, buf, N_KV_B, HEAD_DIM))
```
